```python
import jax, jax.numpy as jnp
from jax import lax
import numpy as np

D_MODEL = 2048
BATCH = 4
SEQ = 2048
DEPTH = 4

CHUNK = 64
Q_BLOCK = 128
N_MIXERS = 2
N_MLA_LAYERS = (DEPTH + 1) // 2
N_FOX_LAYERS = DEPTH // 2

MLA_HEADS = 16
MLA_Q_LORA = 512
MLA_KV_LORA = 512
MLA_NOPE_DIM = 128
MLA_ROPE_DIM = 64
MLA_V_DIM = 128
ROPE_THETA = 10000.0

FOX_HEADS = 16
FOX_HEAD_DIM = D_MODEL // FOX_HEADS

D_FF = -(-8 * D_MODEL // (3 * 256)) * 256

DEEPNORM_ALPHA = (2 * DEPTH) ** 0.25
DEEPNORM_BETA = (8 * DEPTH) ** -0.25
LN_EPS = 1e-5
RMS_EPS = 1e-6

kernel_name = "hybrid_mla_fox_deepnorm_adaln_trunk"


def _layer_norm(x, g, b):
    xf = x.astype(jnp.float32)
    mu = jnp.mean(xf, axis=-1, keepdims=True)
    var = jnp.mean(jnp.square(xf - mu), axis=-1, keepdims=True)
    y = (xf - mu) * lax.rsqrt(var + LN_EPS)
    return (y * g.astype(jnp.float32) + b.astype(jnp.float32)).astype(x.dtype)


def _rms_norm(x, g):
    xf = x.astype(jnp.float32)
    y = xf * lax.rsqrt(jnp.mean(jnp.square(xf), axis=-1, keepdims=True) + RMS_EPS)
    return (y * g.astype(jnp.float32)).astype(x.dtype)


def _rope_cos_sin(positions, dtype):
    inv_freq = ROPE_THETA ** (-jnp.arange(0, MLA_ROPE_DIM, 2, dtype=jnp.float32) / MLA_ROPE_DIM)
    ang = positions.astype(jnp.float32)[..., None] * inv_freq
    return jnp.cos(ang).astype(dtype), jnp.sin(ang).astype(dtype)


def _rope(x, cos, sin):
    x2 = x.reshape(*x.shape[:-1], MLA_ROPE_DIM // 2, 2)
    x0, x1 = x2[..., 0], x2[..., 1]
    out = jnp.stack([x0 * cos - x1 * sin, x0 * sin + x1 * cos], axis=-1)
    return out.reshape(x.shape)


def _mla(h, positions, w_down, q_norm, w_uq, kv_norm, w_uk, w_uv, w_o):
    B, S, _ = h.shape
    H = MLA_HEADS
    lat = h @ w_down
    q_lat, kv_lat, k_pe = jnp.split(lat, [MLA_Q_LORA, MLA_Q_LORA + MLA_KV_LORA], axis=-1)
    q = (_rms_norm(q_lat, q_norm) @ w_uq).reshape(B, S, H, MLA_NOPE_DIM + MLA_ROPE_DIM)
    q_nope, q_pe = q[..., :MLA_NOPE_DIM], q[..., MLA_NOPE_DIM:]
    cos, sin = _rope_cos_sin(positions, h.dtype)
    q_pe = _rope(q_pe, cos[:, :, None, :], sin[:, :, None, :])
    k_pe = _rope(k_pe, cos, sin)
    c_kv = _rms_norm(kv_lat, kv_norm)
    k_nope = (c_kv @ w_uk).reshape(B, S, H, MLA_NOPE_DIM)
    v = (c_kv @ w_uv).reshape(B, S, H, MLA_V_DIM)
    scale = (MLA_NOPE_DIM + MLA_ROPE_DIM) ** -0.5
    chunk_id = jnp.arange(S) // CHUNK
    outs = []
    for qs in range(0, S, Q_BLOCK):
        ke = qs + Q_BLOCK
        s = (jnp.einsum('bqhd,bkhd->bhqk', q_nope[:, qs:ke], k_nope[:, :ke])
             + jnp.einsum('bqhr,bkr->bhqk', q_pe[:, qs:ke], k_pe[:, :ke])).astype(jnp.float32) * scale
        mask = chunk_id[qs:ke, None] >= chunk_id[None, :ke]
        p = jax.nn.softmax(jnp.where(mask, s, -jnp.inf), axis=-1).astype(v.dtype)
        outs.append(jnp.einsum('bhqk,bkhd->bqhd', p, v[:, :ke]))
    o = jnp.concatenate(outs, axis=1).reshape(B, S, H * MLA_V_DIM)
    return o @ w_o


def _fox(h, w_in, b_f, w_o):
    B, S, D = h.shape
    H, Dh = FOX_HEADS, FOX_HEAD_DIM
    proj = h @ w_in
    q, k, v, f_logit = jnp.split(proj, [D, 2 * D, 3 * D], axis=-1)
    q = q.reshape(B, S, H, Dh)
    k = k.reshape(B, S, H, Dh)
    v = v.reshape(B, S, H, Dh)
    log_f = jax.nn.log_sigmoid((f_logit + b_f).astype(jnp.float32))
    cum = jnp.transpose(lax.cumsum(log_f, axis=1), (0, 2, 1))
    scale = Dh ** -0.5
    pos = jnp.arange(S)
    outs = []
    for qs in range(0, S, Q_BLOCK):
        ke = qs + Q_BLOCK
        s = jnp.einsum('bqhd,bkhd->bhqk', q[:, qs:ke], k[:, :ke]).astype(jnp.float32) * scale
        s = s + cum[:, :, qs:ke, None] - cum[:, :, None, :ke]
        mask = pos[qs:ke, None] >= pos[None, :ke]
        p = jax.nn.softmax(jnp.where(mask, s, -jnp.inf), axis=-1).astype(v.dtype)
        outs.append(jnp.einsum('bhqk,bkhd->bqhd', p, v[:, :ke]))
    o = jnp.concatenate(outs, axis=1).reshape(B, S, D)
    return o @ w_o


def _swiglu(h, w1, w3, w2):
    return (jax.nn.silu(h @ w1) * (h @ w3)) @ w2


def _dense(key, shape, fan_in, scale=1.0):
    return jax.random.normal(key, shape, jnp.float32) * (scale * fan_in ** -0.5)


def setup_inputs(seed: int = 0) -> dict:
    key = jax.random.key(seed)
    ks = jax.random.split(key, 32)
    D, L, NA, NB = D_MODEL, DEPTH, N_MLA_LAYERS, N_FOX_LAYERS
    HA, HB = MLA_HEADS, FOX_HEADS
    x = jax.random.normal(ks[0], (BATCH, SEQ, D), jnp.float32)
    c = jax.random.normal(ks[1], (BATCH, D), jnp.float32)
    offset = jax.random.randint(ks[2], (BATCH, 1), 0, 16, dtype=jnp.int32) * CHUNK
    positions = offset + jnp.arange(SEQ, dtype=jnp.int32)[None, :]

    ada_w = _dense(ks[3], (L, D, 6 * D), D, 0.5)
    ada_b = 0.01 * jax.random.normal(ks[4], (L, 6 * D), jnp.float32)
    ln1_g = 1.0 + 0.02 * jax.random.normal(ks[5], (L, D), jnp.float32)
    ln1_b = 0.02 * jax.random.normal(ks[6], (L, D), jnp.float32)
    ln2_g = 1.0 + 0.02 * jax.random.normal(ks[7], (L, D), jnp.float32)
    ln2_b = 0.02 * jax.random.normal(ks[8], (L, D), jnp.float32)
    ffn_w1 = _dense(ks[9], (L, D, D_FF), D)
    ffn_w3 = _dense(ks[10], (L, D, D_FF), D)
    ffn_w2 = _dense(ks[11], (L, D_FF, D), D_FF, DEEPNORM_BETA)

    mla_w_down = _dense(ks[12], (NA, D, MLA_Q_LORA + MLA_KV_LORA + MLA_ROPE_DIM), D)
    mla_q_norm = 1.0 + 0.02 * jax.random.normal(ks[13], (NA, MLA_Q_LORA), jnp.float32)
    mla_w_uq = _dense(ks[14], (NA, MLA_Q_LORA, HA * (MLA_NOPE_DIM + MLA_ROPE_DIM)), MLA_Q_LORA)
    mla_kv_norm = 1.0 + 0.02 * jax.random.normal(ks[15], (NA, MLA_KV_LORA), jnp.float32)
    mla_w_uk = _dense(ks[16], (NA, MLA_KV_LORA, HA * MLA_NOPE_DIM), MLA_KV_LORA)
    mla_w_uv = _dense(ks[17], (NA, MLA_KV_LORA, HA * MLA_V_DIM), MLA_KV_LORA, DEEPNORM_BETA)
    mla_w_o = _dense(ks[18], (NA, HA * MLA_V_DIM, D), HA * MLA_V_DIM, DEEPNORM_BETA)

    fox_wq = _dense(ks[19], (NB, D, D), D)
    fox_wk = _dense(ks[20], (NB, D, D), D)
    fox_wv = _dense(ks[21], (NB, D, D), D, DEEPNORM_BETA)
    fox_wf = _dense(ks[22], (NB, D, HB), D)
    fox_w_in = jnp.concatenate([fox_wq, fox_wk, fox_wv, fox_wf], axis=-1)
    fox_b_f = jax.random.uniform(ks[23], (NB, HB), jnp.float32, 1.0, 4.0)
    fox_w_o = _dense(ks[24], (NB, D, D), D, DEEPNORM_BETA)

    return {
        "x": x, "c": c, "positions": positions,
        "ada_w": ada_w, "ada_b": ada_b,
        "ln1_g": ln1_g, "ln1_b": ln1_b, "ln2_g": ln2_g, "ln2_b": ln2_b,
        "ffn_w1": ffn_w1, "ffn_w3": ffn_w3, "ffn_w2": ffn_w2,
        "mla_w_down": mla_w_down, "mla_q_norm": mla_q_norm, "mla_w_uq": mla_w_uq,
        "mla_kv_norm": mla_kv_norm, "mla_w_uk": mla_w_uk, "mla_w_uv": mla_w_uv, "mla_w_o": mla_w_o,
        "fox_w_in": fox_w_in, "fox_b_f": fox_b_f, "fox_w_o": fox_w_o,
    }


def reference(x, c, positions, ada_w, ada_b, ln1_g, ln1_b, ln2_g, ln2_b,
              ffn_w1, ffn_w3, ffn_w2,
              mla_w_down, mla_q_norm, mla_w_uq, mla_kv_norm, mla_w_uk, mla_w_uv, mla_w_o,
              fox_w_in, fox_b_f, fox_w_o):
    c_act = jax.nn.silu(c)
    for i in range(DEPTH):
        mod = c_act @ ada_w[i] + ada_b[i]
        sh_a, sc_a, g_a, sh_f, sc_f, g_f = [m[:, None, :] for m in jnp.split(mod, 6, axis=-1)]

        h = x * (1.0 + sc_a) + sh_a
        j = i // N_MIXERS
        if i % N_MIXERS == 0:
            y = _mla(h, positions, mla_w_down[j], mla_q_norm[j], mla_w_uq[j],
                     mla_kv_norm[j], mla_w_uk[j], mla_w_uv[j], mla_w_o[j])
        else:
            y = _fox(h, fox_w_in[j], fox_b_f[j], fox_w_o[j])
        x = _layer_norm(DEEPNORM_ALPHA * x + (1.0 + g_a) * y, ln1_g[i], ln1_b[i])

        h = x * (1.0 + sc_f) + sh_f
        y = _swiglu(h, ffn_w1[i], ffn_w3[i], ffn_w2[i])
        x = _layer_norm(DEEPNORM_ALPHA * x + (1.0 + g_f) * y, ln2_g[i], ln2_b[i])
    return x
```

```python
import functools

import jax
import jax.numpy as jnp
import numpy as np
from jax import lax
from jax.experimental import pallas as pl
from jax.experimental.pallas import tpu as pltpu

D_MODEL = 2048
DEPTH = 4
CHUNK = 64
MLA_HEADS = 16
MLA_Q_LORA = 512
MLA_KV_LORA = 512
MLA_NOPE_DIM = 128
MLA_ROPE_DIM = 64
MLA_V_DIM = 128
ROPE_THETA = 10000.0
FOX_HEADS = 16
FOX_HEAD_DIM = D_MODEL // FOX_HEADS
DEEPNORM_ALPHA = (2 * DEPTH) ** 0.25
LN_EPS = 1e-5
RMS_EPS = 1e-6

LANES = 128
VMEM_LIMIT_BYTES = 56 * 1024 * 1024

F32 = jnp.float32
BF16 = jnp.bfloat16


def _params(*sem):
    return pltpu.CompilerParams(dimension_semantics=sem, vmem_limit_bytes=VMEM_LIMIT_BYTES)


def _dot(a, b):
    return jnp.dot(a, b, preferred_element_type=F32)


def _dot_nt(a, b):
    return lax.dot_general(a, b, (((1,), (1,)), ((), ())), preferred_element_type=F32)


def _ada_kernel(c_ref, w_ref, b_ref, o_ref):
    c = c_ref[...]
    c_act = (c * jax.nn.sigmoid(c)).astype(BF16)
    o_ref[...] = _dot(c_act, w_ref[...].astype(BF16)) + b_ref[...]


def _ada_mod(c_pad, ada_w, ada_b, tn=1024):
    L, D, N = ada_w.shape
    rows = c_pad.shape[0]
    return pl.pallas_call(
        _ada_kernel,
        grid=(L, N // tn),
        in_specs=[
            pl.BlockSpec((rows, D), lambda l, j: (0, 0)),
            pl.BlockSpec((None, D, tn), lambda l, j: (l, 0, j)),
            pl.BlockSpec((None, 1, tn), lambda l, j: (l, 0, j)),
        ],
        out_specs=pl.BlockSpec((None, rows, tn), lambda l, j: (l, 0, j)),
        out_shape=jax.ShapeDtypeStruct((L, rows, N), F32),
        compiler_params=_params("parallel", "parallel"),
        name="ada_mod",
    )(c_pad, ada_w, ada_b.reshape(L, 1, N))


def _modulate_kernel(x_ref, sc_ref, sh_ref, h_ref):
    h_ref[...] = (x_ref[...] * (1.0 + sc_ref[...]) + sh_ref[...]).astype(BF16)


def _modulate(x2, sc, sh, seq, bm=1024):
    M, D = x2.shape
    per = seq // bm
    vec = pl.BlockSpec((None, 1, D), lambda i: (i // per, 0, 0))
    return pl.pallas_call(
        _modulate_kernel,
        grid=(M // bm,),
        in_specs=[pl.BlockSpec((bm, D), lambda i: (i, 0)), vec, vec],
        out_specs=pl.BlockSpec((bm, D), lambda i: (i, 0)),
        out_shape=jax.ShapeDtypeStruct((M, D), BF16),
        compiler_params=_params("parallel"),
        name="modulate",
    )(x2, sc, sh)


def _rope_tab_kernel(pos_ref, freq_ref, cos_ref, sina_ref, sinb_ref):
    ang = pos_ref[...].astype(F32) * freq_ref[...]
    lane = lax.broadcasted_iota(jnp.int32, ang.shape, 1)
    half = MLA_ROPE_DIM // 2
    c = jnp.cos(ang)
    s = jnp.sin(ang)
    cos_ref[...] = jnp.where(lane < 2 * half, c, 0.0)
    sina_ref[...] = jnp.where((lane >= half) & (lane < 2 * half), s, 0.0)
    sinb_ref[...] = jnp.where(lane < half, -s, 0.0)


def _rope_tables(pos_col, freq_row, bm=1024):
    M = pos_col.shape[0]
    out = jax.ShapeDtypeStruct((M, LANES), F32)
    spec = pl.BlockSpec((bm, LANES), lambda i: (i, 0))
    return pl.pallas_call(
        _rope_tab_kernel,
        grid=(M // bm,),
        in_specs=[pl.BlockSpec((bm, 1), lambda i: (i, 0)),
                  pl.BlockSpec((1, LANES), lambda i: (0, 0))],
        out_specs=[spec, spec, spec],
        out_shape=[out, out, out],
        compiler_params=_params("parallel"),
        name="rope_tables",
    )(pos_col, freq_row)


def _rope(x, cos, sina, sinb):
    half = MLA_ROPE_DIM // 2
    return (x * cos + pltpu.roll(x, half, 1) * sina
            + pltpu.roll(x, LANES - half, 1) * sinb)


def _rms(x, g):
    return x * lax.rsqrt(jnp.mean(x * x, axis=-1, keepdims=True) + RMS_EPS) * g


def _mla_down_kernel(h_ref, wq_ref, wkv_ref, wpe_ref, qg_ref, kvg_ref,
                     cos_ref, sina_ref, sinb_ref, qn_ref, ckv_ref, kpe_ref):
    h = h_ref[...]
    qn_ref[...] = _rms(_dot(h, wq_ref[...]), qg_ref[...]).astype(BF16)
    ckv_ref[...] = _rms(_dot(h, wkv_ref[...]), kvg_ref[...]).astype(BF16)
    kpe = _dot(h, wpe_ref[...])
    kpe_ref[...] = _rope(kpe, cos_ref[...], sina_ref[...], sinb_ref[...]).astype(BF16)


def _mla_down(h, wq, wkv, wpe, qg, kvg, cos, sina, sinb, bm=512):
    M, D = h.shape
    QL, KL = wq.shape[1], wkv.shape[1]
    row = lambda n: pl.BlockSpec((bm, n), lambda i: (i, 0))
    full = lambda a: pl.BlockSpec(a.shape, lambda i: (0, 0))
    return pl.pallas_call(
        _mla_down_kernel,
        grid=(M // bm,),
        in_specs=[row(D), full(wq), full(wkv), full(wpe), full(qg), full(kvg),
                  row(LANES), row(LANES), row(LANES)],
        out_specs=[row(QL), row(KL), row(LANES)],
        out_shape=[jax.ShapeDtypeStruct((M, QL), BF16),
                   jax.ShapeDtypeStruct((M, KL), BF16),
                   jax.ShapeDtypeStruct((M, LANES), BF16)],
        compiler_params=_params("parallel"),
        name="mla_down",
    )(h, wq, wkv, wpe, qg, kvg, cos, sina, sinb)


def _mla_up_kernel(qn_ref, ckv_ref, wqn_ref, wqp_ref, wk_ref, wv_ref,
                   cos_ref, sina_ref, sinb_ref,
                   qnope_ref, qpe_ref, knope_ref, v_ref, *, scale, heads):
    qn = qn_ref[...]
    ckv = ckv_ref[...]
    qnope_ref[...] = (_dot(qn, wqn_ref[...]) * scale).astype(BF16)
    knope_ref[...] = _dot(ckv, wk_ref[...]).astype(BF16)
    v_ref[...] = _dot(ckv, wv_ref[...]).astype(BF16)
    qpe = _dot(qn, wqp_ref[...])
    cos, sina, sinb = cos_ref[...] * scale, sina_ref[...] * scale, sinb_ref[...] * scale
    for hd in range(heads):
        sl = slice(hd * LANES, (hd + 1) * LANES)
        qpe_ref[:, sl] = _rope(qpe[:, sl], cos, sina, sinb).astype(BF16)


def _mla_up(qn, ckv, wqn, wqp, wk, wv, cos, sina, sinb, scale, bm=512):
    M = qn.shape[0]
    row = lambda n: pl.BlockSpec((bm, n), lambda i: (i, 0))
    full = lambda a: pl.BlockSpec(a.shape, lambda i: (0, 0))
    outs = [wqn.shape[1], wqp.shape[1], wk.shape[1], wv.shape[1]]
    return pl.pallas_call(
        functools.partial(_mla_up_kernel, scale=scale, heads=MLA_HEADS),
        grid=(M // bm,),
        in_specs=[row(qn.shape[1]), row(ckv.shape[1]), full(wqn), full(wqp), full(wk), full(wv),
                  row(LANES), row(LANES), row(LANES)],
        out_specs=[row(n) for n in outs],
        out_shape=[jax.ShapeDtypeStruct((M, n), BF16) for n in outs],
        compiler_params=_params("parallel"),
        name="mla_up",
    )(qn, ckv, wqn, wqp, wk, wv, cos, sina, sinb)


def _attn_kernel(*refs, tq, mask_block, use_pe, use_bias):
    it = iter(refs)
    q_ref = next(it)
    qpe_ref = next(it) if use_pe else None
    k_ref = next(it)
    kpe_ref = next(it) if use_pe else None
    v_ref = next(it)
    cq_ref = next(it) if use_bias else None
    ck_ref = next(it) if use_bias else None
    o_ref = next(it)

    hd = pl.program_id(1)
    qi = pl.program_id(2)
    q = q_ref[...]
    if use_pe:
        q = jnp.concatenate([q, qpe_ref[...]], axis=1)
    if use_bias:
        cq_all = cq_ref[...]
        lane = lax.broadcasted_iota(jnp.int32, cq_all.shape, 1)
        cq = jnp.sum(jnp.where(lane == hd, cq_all, 0.0), axis=1, keepdims=True)

    def scores(kt):
        ks = pl.multiple_of(kt * tq, tq)
        k = k_ref[pl.ds(ks, tq), :]
        if use_pe:
            k = jnp.concatenate([k, kpe_ref[pl.ds(ks, tq), :]], axis=1)
        s = _dot_nt(q, k)
        if use_bias:
            s = s + cq - ck_ref[:, pl.ds(ks, tq)]
        return s, v_ref[pl.ds(ks, tq), :]

    def update(carry, s, v):
        m, l, acc = carry
        m_new = jnp.maximum(m, jnp.max(s, axis=1, keepdims=True))
        p = jnp.exp(s - m_new)
        alpha = jnp.exp(m - m_new)
        l = alpha * l + jnp.sum(p, axis=1, keepdims=True)
        acc = alpha * acc + _dot(p.astype(BF16), v)
        return m_new, l, acc

    def body(kt, carry):
        s, v = scores(kt)
        return update(carry, s, v)

    init = (jnp.full((tq, 1), -jnp.inf, F32), jnp.zeros((tq, 1), F32),
            jnp.zeros((tq, v_ref.shape[1]), F32))
    carry = lax.fori_loop(0, qi, body, init)
    s, v = scores(qi)
    row = lax.broadcasted_iota(jnp.int32, s.shape, 0) // mask_block
    col = lax.broadcasted_iota(jnp.int32, s.shape, 1) // mask_block
    s = jnp.where(row >= col, s, -jnp.inf)
    _, l, acc = update(carry, s, v)
    o_ref[...] = (acc / l).astype(o_ref.dtype)


def _attention(q_arr, q_col, k_arr, k_col, v_arr, v_col, *, batch, seq, heads, mask_block,
               pe=None, bias=None, tq=256):
    nq = seq // tq
    dh = LANES
    in_specs = [pl.BlockSpec((tq, dh), lambda b, h, i: (b * nq + i, q_col + h))]
    args = [q_arr]
    if pe is not None:
        qpe_arr, kpe_arr = pe
        in_specs.append(pl.BlockSpec((tq, dh), lambda b, h, i: (b * nq + i, h)))
        args.append(qpe_arr)
    in_specs.append(pl.BlockSpec((seq, dh), lambda b, h, i: (b, k_col + h)))
    args.append(k_arr)
    if pe is not None:
        in_specs.append(pl.BlockSpec((seq, dh), lambda b, h, i: (b, 0)))
        args.append(kpe_arr)
    in_specs.append(pl.BlockSpec((seq, dh), lambda b, h, i: (b, v_col + h)))
    args.append(v_arr)
    if bias is not None:
        cum_q, cum_k = bias
        in_specs.append(pl.BlockSpec((tq, LANES), lambda b, h, i: (b * nq + i, 0)))
        in_specs.append(pl.BlockSpec((None, 1, seq), lambda b, h, i: (b * LANES + h, 0, 0)))
        args += [cum_q, cum_k.reshape(batch * LANES, 1, seq)]
    return pl.pallas_call(
        functools.partial(_attn_kernel, tq=tq, mask_block=mask_block,
                          use_pe=pe is not None, use_bias=bias is not None),
        grid=(batch, heads, nq),
        in_specs=in_specs,
        out_specs=pl.BlockSpec((tq, dh), lambda b, h, i: (b * nq + i, h)),
        out_shape=jax.ShapeDtypeStruct((batch * seq, heads * dh), BF16),
        compiler_params=_params("parallel", "parallel", "parallel"),
        name="attention",
    )(*args)


def _fox_in_kernel(h_ref, w_ref, o_ref, *, scale, q_tiles):
    j = pl.program_id(0)
    mult = jnp.where(j < q_tiles, scale, 1.0).astype(F32)
    o_ref[...] = (_dot(h_ref[...], w_ref[...]) * mult).astype(BF16)


def _fox_in(h, w, scale, bm=1024, bn=512):
    M, D = h.shape
    N = w.shape[1]
    return pl.pallas_call(
        functools.partial(_fox_in_kernel, scale=scale, q_tiles=D // bn),
        grid=(N // bn, M // bm),
        in_specs=[pl.BlockSpec((bm, D), lambda j, i: (i, 0)),
                  pl.BlockSpec((D, bn), lambda j, i: (0, j))],
        out_specs=pl.BlockSpec((bm, bn), lambda j, i: (i, j)),
        out_shape=jax.ShapeDtypeStruct((M, N), BF16),
        compiler_params=_params("parallel", "parallel"),
        name="fox_in",
    )(h, w)


def _fox_gate_kernel(h_ref, wft_ref, bf_ref, cumq_ref, cumk_ref):
    seq = h_ref.shape[0]
    logit_t = _dot_nt(wft_ref[...], h_ref[...]) + bf_ref[...]
    lf = jnp.minimum(logit_t, 0.0) - jnp.log1p(jnp.exp(-jnp.abs(logit_t)))
    r = lax.broadcasted_iota(jnp.int32, (seq, seq), 0)
    c = lax.broadcasted_iota(jnp.int32, (seq, seq), 1)
    tri = (r <= c).astype(BF16)
    hi = lf.astype(BF16)
    r1 = lf - hi.astype(F32)
    mid = r1.astype(BF16)
    lo = (r1 - mid.astype(F32)).astype(BF16)
    cum_t = _dot(hi, tri) + _dot(mid, tri) + _dot(lo, tri)
    cumk_ref[...] = cum_t
    cumq_ref[...] = cum_t.T


def _fox_gate(h, wft, bf_col, batch, seq):
    D = h.shape[1]
    return pl.pallas_call(
        _fox_gate_kernel,
        grid=(batch,),
        in_specs=[pl.BlockSpec((seq, D), lambda b: (b, 0)),
                  pl.BlockSpec((LANES, D), lambda b: (0, 0)),
                  pl.BlockSpec((LANES, 1), lambda b: (0, 0))],
        out_specs=[pl.BlockSpec((seq, LANES), lambda b: (b, 0)),
                   pl.BlockSpec((None, LANES, seq), lambda b: (b, 0, 0))],
        out_shape=[jax.ShapeDtypeStruct((batch * seq, LANES), F32),
                   jax.ShapeDtypeStruct((batch, LANES, seq), F32)],
        compiler_params=_params("parallel"),
        name="fox_gate",
    )(h, wft, bf_col)


def _ffn_up_kernel(h_ref, w1_ref, w3_ref, g_ref):
    h = h_ref[...]
    a = _dot(h, w1_ref[...])
    b = _dot(h, w3_ref[...])
    g_ref[...] = (a * jax.nn.sigmoid(a) * b).astype(BF16)


def _ffn_up(h, w1, w3, bm=1024, bn=512):
    M, D = h.shape
    N = w1.shape[1]
    wspec = pl.BlockSpec((D, bn), lambda j, i: (0, j))
    return pl.pallas_call(
        _ffn_up_kernel,
        grid=(N // bn, M // bm),
        in_specs=[pl.BlockSpec((bm, D), lambda j, i: (i, 0)), wspec, wspec],
        out_specs=pl.BlockSpec((bm, bn), lambda j, i: (i, j)),
        out_shape=jax.ShapeDtypeStruct((M, N), BF16),
        compiler_params=_params("parallel", "parallel"),
        name="ffn_up",
    )(h, w1, w3)


def _mm_ln_kernel(a_ref, w_ref, x_ref, gate_ref, lg_ref, lb_ref, sc_ref, sh_ref,
                  xo_ref, ho_ref, acc_ref, *, nk):
    k = pl.program_id(1)

    @pl.when(k == 0)
    def _():
        acc_ref[...] = jnp.zeros_like(acc_ref)

    acc_ref[...] += _dot(a_ref[...], w_ref[...])

    @pl.when(k == nk - 1)
    def _():
        z = DEEPNORM_ALPHA * x_ref[...] + (1.0 + gate_ref[...]) * acc_ref[...]
        mu = jnp.mean(z, axis=-1, keepdims=True)
        zc = z - mu
        var = jnp.mean(zc * zc, axis=-1, keepdims=True)
        xn = zc * lax.rsqrt(var + LN_EPS) * lg_ref[...] + lb_ref[...]
        xo_ref[...] = xn
        ho_ref[...] = (xn * (1.0 + sc_ref[...]) + sh_ref[...]).astype(BF16)


def _mm_ln(a, w, x2, gate, ln_g, ln_b, sc, sh, seq, bm=512, bk=512):
    M, K = a.shape
    D = w.shape[1]
    nk = K // bk
    per = seq // bm
    vec = pl.BlockSpec((None, 1, D), lambda i, k: (i // per, 0, 0))
    par = pl.BlockSpec((1, D), lambda i, k: (0, 0))
    row = pl.BlockSpec((bm, D), lambda i, k: (i, 0))
    return pl.pallas_call(
        functools.partial(_mm_ln_kernel, nk=nk),
        grid=(M // bm, nk),
        in_specs=[pl.BlockSpec((bm, bk), lambda i, k: (i, k)),
                  pl.BlockSpec((bk, D), lambda i, k: (k, 0)),
                  row, vec, par, par, vec, vec],
        out_specs=[row, row],
        out_shape=[jax.ShapeDtypeStruct((M, D), F32), jax.ShapeDtypeStruct((M, D), BF16)],
        scratch_shapes=[pltpu.VMEM((bm, D), F32)],
        compiler_params=_params("parallel", "arbitrary"),
        name="matmul_ln",
    )(a, w, x2, gate, ln_g, ln_b, sc, sh)


def _deinterleave_pad(w):
    ev, od = w[..., 0::2], w[..., 1::2]
    pad = jnp.zeros(w.shape[:-1] + (LANES - MLA_ROPE_DIM,), w.dtype)
    return jnp.concatenate([ev, od, pad], axis=-1)


def kernel(x, c, positions, ada_w, ada_b, ln1_g, ln1_b, ln2_g, ln2_b, ffn_w1, ffn_w3, ffn_w2,
           mla_w_down, mla_q_norm, mla_w_uq, mla_kv_norm, mla_w_uk, mla_w_uv, mla_w_o,
           fox_w_in, fox_b_f, fox_w_o):
    B, S, D = x.shape
    M = B * S
    L = ada_w.shape[0]
    H = MLA_HEADS

    c_pad = jnp.concatenate([c, jnp.zeros((8 - B, D), c.dtype)], axis=0)
    mod = _ada_mod(c_pad, ada_w, ada_b)[:, :B].reshape(L, B, 6, 1, D)
    mod = jnp.transpose(mod, (0, 2, 1, 3, 4))

    inv_freq = ROPE_THETA ** (-jnp.arange(0, MLA_ROPE_DIM, 2, dtype=F32) / MLA_ROPE_DIM)
    freq_row = jnp.concatenate(
        [inv_freq, inv_freq, jnp.zeros((LANES - MLA_ROPE_DIM,), F32)]).reshape(1, LANES)
    cos, sina, sinb = _rope_tables(positions.reshape(M, 1), freq_row)

    x2 = x.reshape(M, D)
    h = _modulate(x2, mod[0, 1], mod[0, 0], S)
    mla_scale = (MLA_NOPE_DIM + MLA_ROPE_DIM) ** -0.5
    fox_scale = FOX_HEAD_DIM ** -0.5

    for i in range(L):
        j = i // 2
        if i % 2 == 0:
            QL, KL = MLA_Q_LORA, MLA_KV_LORA
            wd = mla_w_down[j]
            wq = wd[:, :QL].astype(BF16)
            wkv = wd[:, QL:QL + KL].astype(BF16)
            wpe = _deinterleave_pad(wd[:, QL + KL:]).astype(BF16)
            wuq = mla_w_uq[j].reshape(QL, H, MLA_NOPE_DIM + MLA_ROPE_DIM)
            wqn = wuq[..., :MLA_NOPE_DIM].reshape(QL, H * MLA_NOPE_DIM).astype(BF16)
            wqp = _deinterleave_pad(wuq[..., MLA_NOPE_DIM:]).reshape(QL, H * LANES).astype(BF16)
            qn, ckv, kpe = _mla_down(h, wq, wkv, wpe, mla_q_norm[j].reshape(1, QL),
                                     mla_kv_norm[j].reshape(1, KL), cos, sina, sinb)
            qnope, qpe, knope, v = _mla_up(qn, ckv, wqn, wqp, mla_w_uk[j].astype(BF16),
                                           mla_w_uv[j].astype(BF16), cos, sina, sinb, mla_scale)
            o = _attention(qnope, 0, knope, 0, v, 0, batch=B, seq=S, heads=H,
                           mask_block=CHUNK, pe=(qpe, kpe))
            w_o = mla_w_o[j].astype(BF16)
        else:
            w_in = fox_w_in[j]
            qkv = _fox_in(h, w_in[:, :3 * D].astype(BF16), fox_scale)
            wft = jnp.zeros((LANES, D), F32).at[:FOX_HEADS].set(w_in[:, 3 * D:].T).astype(BF16)
            bf_col = jnp.zeros((LANES, 1), F32).at[:FOX_HEADS, 0].set(fox_b_f[j])
            cum_q, cum_k = _fox_gate(h, wft, bf_col, B, S)
            o = _attention(qkv, 0, qkv, FOX_HEADS, qkv, 2 * FOX_HEADS, batch=B, seq=S,
                           heads=FOX_HEADS, mask_block=1, bias=(cum_q, cum_k))
            w_o = fox_w_o[j].astype(BF16)

        x2, h = _mm_ln(o, w_o, x2, mod[i, 2], ln1_g[i].reshape(1, D), ln1_b[i].reshape(1, D),
                       mod[i, 4], mod[i, 3], S)
        g = _ffn_up(h, ffn_w1[i].astype(BF16), ffn_w3[i].astype(BF16))
        nxt = min(i + 1, L - 1)
        x2, h = _mm_ln(g, ffn_w2[i].astype(BF16), x2, mod[i, 5], ln2_g[i].reshape(1, D),
                       ln2_b[i].reshape(1, D), mod[nxt, 1], mod[nxt, 0], S)
    return x2.reshape(B, S, D)
```

```python
import functools

import jax
import jax.numpy as jnp
import numpy as np
from jax import lax
from jax.experimental import pallas as pl
from jax.experimental.pallas import tpu as pltpu

D_MODEL = 2048
DEPTH = 4
CHUNK = 64
MLA_HEADS = 16
MLA_Q_LORA = 512
MLA_KV_LORA = 512
MLA_NOPE_DIM = 128
MLA_ROPE_DIM = 64
MLA_V_DIM = 128
ROPE_THETA = 10000.0
FOX_HEADS = 16
FOX_HEAD_DIM = D_MODEL // FOX_HEADS
DEEPNORM_ALPHA = (2 * DEPTH) ** 0.25
LN_EPS = 1e-5
RMS_EPS = 1e-6

LOG2E = float(np.log2(np.e))
LANES = 128
VMEM_LIMIT_BYTES = 56 * 1024 * 1024

F32 = jnp.float32
BF16 = jnp.bfloat16


def _params(*sem):
    return pltpu.CompilerParams(dimension_semantics=sem, vmem_limit_bytes=VMEM_LIMIT_BYTES)


def _dot(a, b):
    return jnp.dot(a, b, preferred_element_type=F32)


def _dot_nt(a, b):
    return lax.dot_general(a, b, (((1,), (1,)), ((), ())), preferred_element_type=F32)


def _ada_kernel(c_ref, w_ref, b_ref, o_ref):
    c = c_ref[...]
    c_act = (c * jax.nn.sigmoid(c)).astype(BF16)
    o_ref[...] = _dot(c_act, w_ref[...].astype(BF16)) + b_ref[...]


def _ada_mod(c_pad, ada_w, ada_b, tn=1024):
    L, D, N = ada_w.shape
    rows = c_pad.shape[0]
    return pl.pallas_call(
        _ada_kernel,
        grid=(L, N // tn),
        in_specs=[
            pl.BlockSpec((rows, D), lambda l, j: (0, 0)),
            pl.BlockSpec((None, D, tn), lambda l, j: (l, 0, j)),
            pl.BlockSpec((None, 1, tn), lambda l, j: (l, 0, j)),
        ],
        out_specs=pl.BlockSpec((None, rows, tn), lambda l, j: (l, 0, j)),
        out_shape=jax.ShapeDtypeStruct((L, rows, N), F32),
        compiler_params=_params("parallel", "parallel"),
        name="ada_mod",
    )(c_pad, ada_w, ada_b.reshape(L, 1, N))


def _modulate_kernel(x_ref, sc_ref, sh_ref, h_ref):
    h_ref[...] = (x_ref[...] * (1.0 + sc_ref[...]) + sh_ref[...]).astype(BF16)


def _modulate(x2, sc, sh, seq, bm=1024):
    M, D = x2.shape
    per = seq // bm
    vec = pl.BlockSpec((None, 1, D), lambda i: (i // per, 0, 0))
    return pl.pallas_call(
        _modulate_kernel,
        grid=(M // bm,),
        in_specs=[pl.BlockSpec((bm, D), lambda i: (i, 0)), vec, vec],
        out_specs=pl.BlockSpec((bm, D), lambda i: (i, 0)),
        out_shape=jax.ShapeDtypeStruct((M, D), BF16),
        compiler_params=_params("parallel"),
        name="modulate",
    )(x2, sc, sh)


def _rope_tab_kernel(pos_ref, freq_ref, cos_ref, sina_ref, sinb_ref):
    ang = pos_ref[...].astype(F32) * freq_ref[...]
    lane = lax.broadcasted_iota(jnp.int32, ang.shape, 1)
    half = MLA_ROPE_DIM // 2
    c = jnp.cos(ang)
    s = jnp.sin(ang)
    cos_ref[...] = jnp.where(lane < 2 * half, c, 0.0)
    sina_ref[...] = jnp.where((lane >= half) & (lane < 2 * half), s, 0.0)
    sinb_ref[...] = jnp.where(lane < half, -s, 0.0)


def _rope_tables(pos_col, freq_row, bm=1024):
    M = pos_col.shape[0]
    out = jax.ShapeDtypeStruct((M, LANES), F32)
    spec = pl.BlockSpec((bm, LANES), lambda i: (i, 0))
    return pl.pallas_call(
        _rope_tab_kernel,
        grid=(M // bm,),
        in_specs=[pl.BlockSpec((bm, 1), lambda i: (i, 0)),
                  pl.BlockSpec((1, LANES), lambda i: (0, 0))],
        out_specs=[spec, spec, spec],
        out_shape=[out, out, out],
        compiler_params=_params("parallel"),
        name="rope_tables",
    )(pos_col, freq_row)


def _rope(x, cos, sina, sinb):
    half = MLA_ROPE_DIM // 2
    return (x * cos + pltpu.roll(x, half, 1) * sina
            + pltpu.roll(x, LANES - half, 1) * sinb)


def _rms(x, g):
    return x * lax.rsqrt(jnp.mean(x * x, axis=-1, keepdims=True) + RMS_EPS) * g


def _mla_down_kernel(h_ref, wq_ref, wkv_ref, wpe_ref, qg_ref, kvg_ref,
                     cos_ref, sina_ref, sinb_ref, qn_ref, ckv_ref, kpe_ref):
    h = h_ref[...]
    qn_ref[...] = _rms(_dot(h, wq_ref[...]), qg_ref[...]).astype(BF16)
    ckv_ref[...] = _rms(_dot(h, wkv_ref[...]), kvg_ref[...]).astype(BF16)
    kpe = _dot(h, wpe_ref[...])
    kpe_ref[...] = _rope(kpe, cos_ref[...], sina_ref[...], sinb_ref[...]).astype(BF16)


def _mla_down(h, wq, wkv, wpe, qg, kvg, cos, sina, sinb, bm=512):
    M, D = h.shape
    QL, KL = wq.shape[1], wkv.shape[1]
    row = lambda n: pl.BlockSpec((bm, n), lambda i: (i, 0))
    full = lambda a: pl.BlockSpec(a.shape, lambda i: (0, 0))
    return pl.pallas_call(
        _mla_down_kernel,
        grid=(M // bm,),
        in_specs=[row(D), full(wq), full(wkv), full(wpe), full(qg), full(kvg),
                  row(LANES), row(LANES), row(LANES)],
        out_specs=[row(QL), row(KL), row(LANES)],
        out_shape=[jax.ShapeDtypeStruct((M, QL), BF16),
                   jax.ShapeDtypeStruct((M, KL), BF16),
                   jax.ShapeDtypeStruct((M, LANES), BF16)],
        compiler_params=_params("parallel"),
        name="mla_down",
    )(h, wq, wkv, wpe, qg, kvg, cos, sina, sinb)


def _mla_up_kernel(qn_ref, ckv_ref, wqn_ref, wqp_ref, wk_ref, wv_ref,
                   cos_ref, sina_ref, sinb_ref,
                   qnope_ref, qpe_ref, knope_ref, v_ref, *, scale, heads):
    qn = qn_ref[...]
    ckv = ckv_ref[...]
    qnope_ref[...] = (_dot(qn, wqn_ref[...]) * scale).astype(BF16)
    knope_ref[...] = _dot(ckv, wk_ref[...]).astype(BF16)
    v_ref[...] = _dot(ckv, wv_ref[...]).astype(BF16)
    qpe = _dot(qn, wqp_ref[...])
    cos, sina, sinb = cos_ref[...] * scale, sina_ref[...] * scale, sinb_ref[...] * scale
    for hd in range(heads):
        sl = slice(hd * LANES, (hd + 1) * LANES)
        qpe_ref[:, sl] = _rope(qpe[:, sl], cos, sina, sinb).astype(BF16)


def _mla_up(qn, ckv, wqn, wqp, wk, wv, cos, sina, sinb, scale, bm=512):
    M = qn.shape[0]
    row = lambda n: pl.BlockSpec((bm, n), lambda i: (i, 0))
    full = lambda a: pl.BlockSpec(a.shape, lambda i: (0, 0))
    outs = [wqn.shape[1], wqp.shape[1], wk.shape[1], wv.shape[1]]
    return pl.pallas_call(
        functools.partial(_mla_up_kernel, scale=scale, heads=MLA_HEADS),
        grid=(M // bm,),
        in_specs=[row(qn.shape[1]), row(ckv.shape[1]), full(wqn), full(wqp), full(wk), full(wv),
                  row(LANES), row(LANES), row(LANES)],
        out_specs=[row(n) for n in outs],
        out_shape=[jax.ShapeDtypeStruct((M, n), BF16) for n in outs],
        compiler_params=_params("parallel"),
        name="mla_up",
    )(qn, ckv, wqn, wqp, wk, wv, cos, sina, sinb)


def _attn_kernel(q_ref, qe_ref, k_ref, ke_ref, v_ref, o_ref, *, tq, mask_block, ext_group):
    seq = q_ref.shape[0]
    ke = ke_ref[...]
    if ext_group:
        hd = pl.program_id(1)
        lane = lax.broadcasted_iota(jnp.int32, ke.shape, 1)
        keep = (lane >= hd * ext_group) & (lane < (hd + 1) * ext_group)
        ke = jnp.where(keep, ke, jnp.zeros_like(ke))
    k = jnp.concatenate([k_ref[...], ke], axis=1)
    row = lax.broadcasted_iota(jnp.int32, (tq, tq), 0) // mask_block
    col = lax.broadcasted_iota(jnp.int32, (tq, tq), 1) // mask_block
    visible = row >= col
    for i in range(seq // tq):
        lo, hi = i * tq, (i + 1) * tq
        q = jnp.concatenate([q_ref[lo:hi, :], qe_ref[lo:hi, :]], axis=1)
        s_d = jnp.where(visible, _dot_nt(q, k[lo:hi]), -jnp.inf)
        m = jnp.max(s_d, axis=1, keepdims=True)
        if i:
            s_a = _dot_nt(q, k[:lo])
            m = jnp.maximum(m, jnp.max(s_a, axis=1, keepdims=True))
        p_d = jnp.exp2(s_d - m)
        l = jnp.sum(p_d, axis=1, keepdims=True)
        acc = _dot(p_d.astype(BF16), v_ref[lo:hi, :])
        if i:
            p_a = jnp.exp2(s_a - m)
            l = l + jnp.sum(p_a, axis=1, keepdims=True)
            acc = acc + _dot(p_a.astype(BF16), v_ref[:lo, :])
        o_ref[lo:hi, :] = (acc * (1.0 / l)).astype(o_ref.dtype)


def _attention(q, qe, k, ke, v, *, batch, seq, heads, mask_block, ext_group, tq=256):
    def spec(col):
        return pl.BlockSpec((seq, LANES), lambda b, h: (b, col(h)))
    ops = [q, qe, k, ke, v]
    return pl.pallas_call(
        functools.partial(_attn_kernel, tq=tq, mask_block=mask_block, ext_group=ext_group),
        grid=(batch, heads),
        in_specs=[spec(col) for _, col in ops],
        out_specs=pl.BlockSpec((seq, LANES), lambda b, h: (b, h)),
        out_shape=jax.ShapeDtypeStruct((batch * seq, heads * LANES), BF16),
        compiler_params=_params("parallel", "parallel"),
        name="attention",
    )(*[a for a, _ in ops])


def _fox_in_kernel(h_ref, w_ref, o_ref, wb_ref, *, scale, q_tiles):
    j = pl.program_id(0)

    @pl.when(pl.program_id(1) == 0)
    def _():
        wb_ref[...] = w_ref[...].astype(BF16)

    mult = jnp.where(j < q_tiles, scale, 1.0).astype(F32)
    o_ref[...] = (_dot(h_ref[...], wb_ref[...]) * mult).astype(BF16)


def _fox_in(h, w_stack, layer, n_out, scale, bm=1024, bn=512):
    M, D = h.shape
    return pl.pallas_call(
        functools.partial(_fox_in_kernel, scale=scale, q_tiles=D // bn),
        grid=(n_out // bn, M // bm),
        in_specs=[pl.BlockSpec((bm, D), lambda j, i: (i, 0)),
                  pl.BlockSpec((None, D, bn), lambda j, i: (layer, 0, j))],
        out_specs=pl.BlockSpec((bm, bn), lambda j, i: (i, j)),
        out_shape=jax.ShapeDtypeStruct((M, n_out), BF16),
        scratch_shapes=[pltpu.VMEM((D, bn), BF16)],
        compiler_params=_params("parallel", "arbitrary"),
        name="fox_in",
    )(h, w_stack)


FOX_EXT = 6


def _split3(x):
    hi = x.astype(BF16)
    r1 = x - hi.astype(F32)
    mid = r1.astype(BF16)
    return hi, mid, (r1 - mid.astype(F32)).astype(BF16)


def _fox_gate_kernel(h_ref, wft_ref, bf_ref, aq_ref, ak_ref):
    seq = h_ref.shape[0]
    logit_t = _dot_nt(wft_ref[...], h_ref[...]) + bf_ref[...]
    lf = jnp.minimum(logit_t, 0.0) - jnp.log1p(jnp.exp(-jnp.abs(logit_t)))
    r = lax.broadcasted_iota(jnp.int32, (seq, seq), 0)
    c = lax.broadcasted_iota(jnp.int32, (seq, seq), 1)
    tri = (r <= c).astype(BF16)
    cum_t = sum(_dot(t, tri) for t in _split3(lf))
    terms = _split3(cum_t.T * LOG2E)
    src = lax.broadcasted_iota(jnp.int32, (LANES, LANES), 0)
    dst = lax.broadcasted_iota(jnp.int32, (LANES, LANES), 1)
    lane = lax.broadcasted_iota(jnp.int32, (1, LANES), 1)
    used = lane < FOX_HEADS * FOX_EXT
    aq = jnp.where(used & (lane % FOX_EXT >= 3), 1.0, 0.0)
    ak = jnp.where(used & (lane % FOX_EXT < 3), 1.0, 0.0)
    head = src < FOX_HEADS
    for t, term in enumerate(terms):
        aq = aq + _dot(term, (head & (dst == src * FOX_EXT + t)).astype(BF16))
        ak = ak - _dot(term, (head & (dst == src * FOX_EXT + 3 + t)).astype(BF16))
    aq_ref[...] = aq.astype(BF16)
    ak_ref[...] = ak.astype(BF16)


def _fox_gate(h, wft, bf_col, batch, seq):
    D = h.shape[1]
    out = jax.ShapeDtypeStruct((batch * seq, LANES), BF16)
    return pl.pallas_call(
        _fox_gate_kernel,
        grid=(batch,),
        in_specs=[pl.BlockSpec((seq, D), lambda b: (b, 0)),
                  pl.BlockSpec((LANES, D), lambda b: (0, 0)),
                  pl.BlockSpec((LANES, 1), lambda b: (0, 0))],
        out_specs=[pl.BlockSpec((seq, LANES), lambda b: (b, 0))] * 2,
        out_shape=[out, out],
        compiler_params=_params("parallel"),
        name="fox_gate",
    )(h, wft, bf_col)


def _ffn_up_kernel(h_ref, w1_ref, w3_ref, g_ref, w1b_ref, w3b_ref):
    @pl.when(pl.program_id(1) == 0)
    def _():
        w1b_ref[...] = w1_ref[...].astype(BF16)
        w3b_ref[...] = w3_ref[...].astype(BF16)

    h = h_ref[...]
    a = _dot(h, w1b_ref[...])
    b = _dot(h, w3b_ref[...])
    g_ref[...] = (a * jax.nn.sigmoid(a) * b).astype(BF16)


def _ffn_up(h, w1_stack, w3_stack, layer, bm=1024, bn=512):
    M, D = h.shape
    N = w1_stack.shape[2]
    wspec = pl.BlockSpec((None, D, bn), lambda j, i: (layer, 0, j))
    return pl.pallas_call(
        _ffn_up_kernel,
        grid=(N // bn, M // bm),
        in_specs=[pl.BlockSpec((bm, D), lambda j, i: (i, 0)), wspec, wspec],
        out_specs=pl.BlockSpec((bm, bn), lambda j, i: (i, j)),
        out_shape=jax.ShapeDtypeStruct((M, N), BF16),
        scratch_shapes=[pltpu.VMEM((D, bn), BF16), pltpu.VMEM((D, bn), BF16)],
        compiler_params=_params("parallel", "arbitrary"),
        name="ffn_up",
    )(h, w1_stack, w3_stack)


def _mm_ln_kernel(a_ref, w_ref, x_ref, gate_ref, lg_ref, lb_ref, sc_ref, sh_ref,
                  xo_ref, ho_ref, *, nk):
    k = pl.program_id(1)

    @pl.when(k == 0)
    def _():
        xo_ref[...] = _dot(a_ref[...], w_ref[...].astype(BF16))

    @pl.when(k > 0)
    def _():
        xo_ref[...] += _dot(a_ref[...], w_ref[...].astype(BF16))

    @pl.when(k == nk - 1)
    def _():
        z = DEEPNORM_ALPHA * x_ref[...] + (1.0 + gate_ref[...]) * xo_ref[...]
        mu = jnp.mean(z, axis=-1, keepdims=True)
        zc = z - mu
        var = jnp.mean(zc * zc, axis=-1, keepdims=True)
        xn = zc * lax.rsqrt(var + LN_EPS) * lg_ref[...] + lb_ref[...]
        xo_ref[...] = xn
        ho_ref[...] = (xn * (1.0 + sc_ref[...]) + sh_ref[...]).astype(BF16)


def _mm_ln(a, w_stack, layer, x2, gate, ln_g, ln_b, sc, sh, seq, bm=1024, bk=512):
    M, K = a.shape
    D = w_stack.shape[2]
    nk = K // bk
    per = seq // bm
    once = pl.Buffered(1)
    vec = pl.BlockSpec((None, 1, D), lambda i, k: (i // per, 0, 0))
    par = pl.BlockSpec((1, D), lambda i, k: (0, 0))
    row = lambda: pl.BlockSpec((bm, D), lambda i, k: (i, 0), pipeline_mode=once)
    return pl.pallas_call(
        functools.partial(_mm_ln_kernel, nk=nk),
        grid=(M // bm, nk),
        in_specs=[pl.BlockSpec((bm, bk), lambda i, k: (i, k)),
                  pl.BlockSpec((None, bk, D), lambda i, k: (layer, k, 0)),
                  row(), vec, par, par, vec, vec],
        out_specs=[row(), row()],
        out_shape=[jax.ShapeDtypeStruct((M, D), F32), jax.ShapeDtypeStruct((M, D), BF16)],
        compiler_params=_params("parallel", "arbitrary"),
        name="matmul_ln",
    )(a, w_stack, x2, gate, ln_g, ln_b, sc, sh)


def _deinterleave_pad(w):
    ev, od = w[..., 0::2], w[..., 1::2]
    pad = jnp.zeros(w.shape[:-1] + (LANES - MLA_ROPE_DIM,), w.dtype)
    return jnp.concatenate([ev, od, pad], axis=-1)


def kernel(x, c, positions, ada_w, ada_b, ln1_g, ln1_b, ln2_g, ln2_b, ffn_w1, ffn_w3, ffn_w2,
           mla_w_down, mla_q_norm, mla_w_uq, mla_kv_norm, mla_w_uk, mla_w_uv, mla_w_o,
           fox_w_in, fox_b_f, fox_w_o):
    B, S, D = x.shape
    M = B * S
    L = ada_w.shape[0]
    H = MLA_HEADS

    c_pad = jnp.concatenate([c, jnp.zeros((8 - B, D), c.dtype)], axis=0)
    mod = _ada_mod(c_pad, ada_w, ada_b)[:, :B].reshape(L, B, 6, 1, D)
    mod = jnp.transpose(mod, (0, 2, 1, 3, 4))

    inv_freq = ROPE_THETA ** (-jnp.arange(0, MLA_ROPE_DIM, 2, dtype=F32) / MLA_ROPE_DIM)
    freq_row = jnp.concatenate(
        [inv_freq, inv_freq, jnp.zeros((LANES - MLA_ROPE_DIM,), F32)]).reshape(1, LANES)
    cos, sina, sinb = _rope_tables(positions.reshape(M, 1), freq_row)

    x2 = x.reshape(M, D)
    h = _modulate(x2, mod[0, 1], mod[0, 0], S)
    mla_scale = (MLA_NOPE_DIM + MLA_ROPE_DIM) ** -0.5 * LOG2E
    fox_scale = FOX_HEAD_DIM ** -0.5 * LOG2E
    head_col = lambda hd: hd
    shared_col = lambda hd: 0

    for i in range(L):
        j = i // 2
        if i % 2 == 0:
            QL, KL = MLA_Q_LORA, MLA_KV_LORA
            wd = mla_w_down[j]
            wq = wd[:, :QL].astype(BF16)
            wkv = wd[:, QL:QL + KL].astype(BF16)
            wpe = _deinterleave_pad(wd[:, QL + KL:]).astype(BF16)
            wuq = mla_w_uq[j].reshape(QL, H, MLA_NOPE_DIM + MLA_ROPE_DIM)
            wqn = wuq[..., :MLA_NOPE_DIM].reshape(QL, H * MLA_NOPE_DIM).astype(BF16)
            wqp = _deinterleave_pad(wuq[..., MLA_NOPE_DIM:]).reshape(QL, H * LANES).astype(BF16)
            qn, ckv, kpe = _mla_down(h, wq, wkv, wpe, mla_q_norm[j].reshape(1, QL),
                                     mla_kv_norm[j].reshape(1, KL), cos, sina, sinb)
            qnope, qpe, knope, v = _mla_up(qn, ckv, wqn, wqp, mla_w_uk[j].astype(BF16),
                                           mla_w_uv[j].astype(BF16), cos, sina, sinb, mla_scale)
            o = _attention((qnope, head_col), (qpe, head_col), (knope, head_col),
                           (kpe, shared_col), (v, head_col), batch=B, seq=S, heads=H,
                           mask_block=CHUNK, ext_group=0)
            w_o = mla_w_o
        else:
            qkv = _fox_in(h, fox_w_in, j, 3 * D, fox_scale)
            wf = fox_w_in[j, :, 3 * D:]
            wft = jnp.zeros((LANES, D), F32).at[:FOX_HEADS].set(wf.T).astype(BF16)
            bf_col = jnp.zeros((LANES, 1), F32).at[:FOX_HEADS, 0].set(fox_b_f[j])
            aq, ak = _fox_gate(h, wft, bf_col, B, S)
            o = _attention((qkv, head_col), (aq, shared_col),
                           (qkv, lambda hd: FOX_HEADS + hd), (ak, shared_col),
                           (qkv, lambda hd: 2 * FOX_HEADS + hd), batch=B, seq=S,
                           heads=FOX_HEADS, mask_block=1, ext_group=FOX_EXT)
            w_o = fox_w_o

        x2, h = _mm_ln(o, w_o, j, x2, mod[i, 2], ln1_g[i].reshape(1, D), ln1_b[i].reshape(1, D),
                       mod[i, 4], mod[i, 3], S)
        g = _ffn_up(h, ffn_w1, ffn_w3, i)
        nxt = min(i + 1, L - 1)
        x2, h = _mm_ln(g, ffn_w2, i, x2, mod[i, 5], ln2_g[i].reshape(1, D),
                       ln2_b[i].reshape(1, D), mod[nxt, 1], mod[nxt, 0], S)
    return x2.reshape(B, S, D)
```

```python
import functools

import jax
import jax.numpy as jnp
import numpy as np
from jax import lax
from jax.experimental import pallas as pl
from jax.experimental.pallas import tpu as pltpu

D_MODEL = 2048
DEPTH = 4
CHUNK = 64
MLA_HEADS = 16
MLA_Q_LORA = 512
MLA_KV_LORA = 512
MLA_NOPE_DIM = 128
MLA_ROPE_DIM = 64
MLA_V_DIM = 128
ROPE_THETA = 10000.0
FOX_HEADS = 16
FOX_HEAD_DIM = D_MODEL // FOX_HEADS
DEEPNORM_ALPHA = (2 * DEPTH) ** 0.25
LN_EPS = 1e-5
RMS_EPS = 1e-6

LOG2E = float(np.log2(np.e))
LANES = 128
VMEM_LIMIT_BYTES = 56 * 1024 * 1024

F32 = jnp.float32
BF16 = jnp.bfloat16


def _params(*sem):
    return pltpu.CompilerParams(dimension_semantics=sem, vmem_limit_bytes=VMEM_LIMIT_BYTES)


def _dot(a, b):
    return jnp.dot(a, b, preferred_element_type=F32)


def _dot_nt(a, b):
    return lax.dot_general(a, b, (((1,), (1,)), ((), ())), preferred_element_type=F32)


def _ada_kernel(c_ref, w_ref, b_ref, o_ref):
    c = c_ref[...]
    c_act = (c * jax.nn.sigmoid(c)).astype(BF16)
    o_ref[...] = _dot(c_act, w_ref[...].astype(BF16)) + b_ref[...]


def _ada_mod(c_pad, ada_w, ada_b, tn=1024):
    L, D, N = ada_w.shape
    rows = c_pad.shape[0]
    return pl.pallas_call(
        _ada_kernel,
        grid=(L, N // tn),
        in_specs=[
            pl.BlockSpec((rows, D), lambda l, j: (0, 0)),
            pl.BlockSpec((None, D, tn), lambda l, j: (l, 0, j)),
            pl.BlockSpec((None, 1, tn), lambda l, j: (l, 0, j)),
        ],
        out_specs=pl.BlockSpec((None, rows, tn), lambda l, j: (l, 0, j)),
        out_shape=jax.ShapeDtypeStruct((L, rows, N), F32),
        compiler_params=_params("parallel", "parallel"),
        name="ada_mod",
    )(c_pad, ada_w, ada_b.reshape(L, 1, N))


def _modulate_kernel(x_ref, sc_ref, sh_ref, h_ref):
    h_ref[...] = (x_ref[...] * (1.0 + sc_ref[...]) + sh_ref[...]).astype(BF16)


def _modulate(x2, sc, sh, seq, bm=1024):
    M, D = x2.shape
    per = seq // bm
    vec = pl.BlockSpec((None, 1, D), lambda i: (i // per, 0, 0))
    return pl.pallas_call(
        _modulate_kernel,
        grid=(M // bm,),
        in_specs=[pl.BlockSpec((bm, D), lambda i: (i, 0)), vec, vec],
        out_specs=pl.BlockSpec((bm, D), lambda i: (i, 0)),
        out_shape=jax.ShapeDtypeStruct((M, D), BF16),
        compiler_params=_params("parallel"),
        name="modulate",
    )(x2, sc, sh)


def _rope_tab_kernel(pos_ref, freq_ref, cos_ref, sina_ref, sinb_ref):
    ang = pos_ref[...].astype(F32) * freq_ref[...]
    lane = lax.broadcasted_iota(jnp.int32, ang.shape, 1)
    half = MLA_ROPE_DIM // 2
    c = jnp.cos(ang)
    s = jnp.sin(ang)
    cos_ref[...] = jnp.where(lane < 2 * half, c, 0.0)
    sina_ref[...] = jnp.where((lane >= half) & (lane < 2 * half), s, 0.0)
    sinb_ref[...] = jnp.where(lane < half, -s, 0.0)


def _rope_tables(pos_col, freq_row, bm=1024):
    M = pos_col.shape[0]
    out = jax.ShapeDtypeStruct((M, LANES), F32)
    spec = pl.BlockSpec((bm, LANES), lambda i: (i, 0))
    return pl.pallas_call(
        _rope_tab_kernel,
        grid=(M // bm,),
        in_specs=[pl.BlockSpec((bm, 1), lambda i: (i, 0)),
                  pl.BlockSpec((1, LANES), lambda i: (0, 0))],
        out_specs=[spec, spec, spec],
        out_shape=[out, out, out],
        compiler_params=_params("parallel"),
        name="rope_tables",
    )(pos_col, freq_row)


def _rope(x, cos, sina, sinb):
    half = MLA_ROPE_DIM // 2
    return (x * cos + pltpu.roll(x, half, 1) * sina
            + pltpu.roll(x, LANES - half, 1) * sinb)


def _rms(x, g):
    return x * lax.rsqrt(jnp.mean(x * x, axis=-1, keepdims=True) + RMS_EPS) * g


def _mla_down_kernel(h_ref, wq_ref, wkv_ref, wpe_ref, qg_ref, kvg_ref,
                     cos_ref, sina_ref, sinb_ref, qn_ref, ckv_ref, kpe_ref):
    h = h_ref[...]
    qn_ref[...] = _rms(_dot(h, wq_ref[...]), qg_ref[...]).astype(BF16)
    ckv_ref[...] = _rms(_dot(h, wkv_ref[...]), kvg_ref[...]).astype(BF16)
    kpe = _dot(h, wpe_ref[...])
    kpe_ref[...] = _rope(kpe, cos_ref[...], sina_ref[...], sinb_ref[...]).astype(BF16)


def _mla_down(h, wq, wkv, wpe, qg, kvg, cos, sina, sinb, bm=512):
    M, D = h.shape
    QL, KL = wq.shape[1], wkv.shape[1]
    row = lambda n: pl.BlockSpec((bm, n), lambda i: (i, 0))
    full = lambda a: pl.BlockSpec(a.shape, lambda i: (0, 0))
    return pl.pallas_call(
        _mla_down_kernel,
        grid=(M // bm,),
        in_specs=[row(D), full(wq), full(wkv), full(wpe), full(qg), full(kvg),
                  row(LANES), row(LANES), row(LANES)],
        out_specs=[row(QL), row(KL), row(LANES)],
        out_shape=[jax.ShapeDtypeStruct((M, QL), BF16),
                   jax.ShapeDtypeStruct((M, KL), BF16),
                   jax.ShapeDtypeStruct((M, LANES), BF16)],
        compiler_params=_params("parallel"),
        name="mla_down",
    )(h, wq, wkv, wpe, qg, kvg, cos, sina, sinb)


def _mla_up_kernel(qn_ref, ckv_ref, wqn_ref, wqp_ref, wk_ref, wv_ref,
                   cos_ref, sina_ref, sinb_ref,
                   qnope_ref, qpe_ref, knope_ref, v_ref, *, scale, heads):
    qn = qn_ref[...]
    ckv = ckv_ref[...]
    qnope_ref[...] = (_dot(qn, wqn_ref[...]) * scale).astype(BF16)
    knope_ref[...] = _dot(ckv, wk_ref[...]).astype(BF16)
    v_ref[...] = _dot(ckv, wv_ref[...]).astype(BF16)
    qpe = _dot(qn, wqp_ref[...])
    cos, sina, sinb = cos_ref[...] * scale, sina_ref[...] * scale, sinb_ref[...] * scale
    for hd in range(heads):
        sl = slice(hd * LANES, (hd + 1) * LANES)
        qpe_ref[:, sl] = _rope(qpe[:, sl], cos, sina, sinb).astype(BF16)


def _mla_up(qn, ckv, wqn, wqp, wk, wv, cos, sina, sinb, scale, bm=512):
    M = qn.shape[0]
    row = lambda n: pl.BlockSpec((bm, n), lambda i: (i, 0))
    full = lambda a: pl.BlockSpec(a.shape, lambda i: (0, 0))
    outs = [wqn.shape[1], wqp.shape[1], wk.shape[1], wv.shape[1]]
    return pl.pallas_call(
        functools.partial(_mla_up_kernel, scale=scale, heads=MLA_HEADS),
        grid=(M // bm,),
        in_specs=[row(qn.shape[1]), row(ckv.shape[1]), full(wqn), full(wqp), full(wk), full(wv),
                  row(LANES), row(LANES), row(LANES)],
        out_specs=[row(n) for n in outs],
        out_shape=[jax.ShapeDtypeStruct((M, n), BF16) for n in outs],
        compiler_params=_params("parallel"),
        name="mla_up",
    )(qn, ckv, wqn, wqp, wk, wv, cos, sina, sinb)


def _attn_kernel(q_ref, qe_ref, k_ref, ke_ref, v_ref, o_ref, *, tq, mask_block, ext_group):
    seq = q_ref.shape[0]
    ke = ke_ref[...]
    if ext_group:
        hd = pl.program_id(1)
        lane = lax.broadcasted_iota(jnp.int32, ke.shape, 1)
        keep = (lane >= hd * ext_group) & (lane < (hd + 1) * ext_group)
        ke = jnp.where(keep, ke, jnp.zeros_like(ke))
    k = jnp.concatenate([k_ref[...], ke], axis=1)
    row = lax.broadcasted_iota(jnp.int32, (tq, tq), 0) // mask_block
    col = lax.broadcasted_iota(jnp.int32, (tq, tq), 1) // mask_block
    visible = row >= col
    for i in range(seq // tq):
        lo, hi = i * tq, (i + 1) * tq
        q = jnp.concatenate([q_ref[lo:hi, :], qe_ref[lo:hi, :]], axis=1)
        s_d = jnp.where(visible, _dot_nt(q, k[lo:hi]), -jnp.inf)
        m = jnp.max(s_d, axis=1, keepdims=True)
        if i:
            s_a = _dot_nt(q, k[:lo])
            m = jnp.maximum(m, jnp.max(s_a, axis=1, keepdims=True))
        p_d = jnp.exp2(s_d - m)
        l = jnp.sum(p_d, axis=1, keepdims=True)
        acc = _dot(p_d.astype(BF16), v_ref[lo:hi, :])
        if i:
            p_a = jnp.exp2(s_a - m)
            l = l + jnp.sum(p_a, axis=1, keepdims=True)
            acc = acc + _dot(p_a.astype(BF16), v_ref[:lo, :])
        o_ref[lo:hi, :] = (acc * (1.0 / l)).astype(o_ref.dtype)


def _attention(q, qe, k, ke, v, *, batch, seq, heads, mask_block, ext_group, tq=256):
    def spec(col):
        return pl.BlockSpec((seq, LANES), lambda b, h: (b, col(h)))
    ops = [q, qe, k, ke, v]
    return pl.pallas_call(
        functools.partial(_attn_kernel, tq=tq, mask_block=mask_block, ext_group=ext_group),
        grid=(batch, heads),
        in_specs=[spec(col) for _, col in ops],
        out_specs=pl.BlockSpec((seq, LANES), lambda b, h: (b, h)),
        out_shape=jax.ShapeDtypeStruct((batch * seq, heads * LANES), BF16),
        compiler_params=_params("parallel", "parallel"),
        name="attention",
    )(*[a for a, _ in ops])


def _fox_in_kernel(h_ref, w_ref, o_ref, wb_ref, *, scale, q_tiles):
    j = pl.program_id(0)

    @pl.when(pl.program_id(1) == 0)
    def _():
        wb_ref[...] = w_ref[...].astype(BF16)

    mult = jnp.where(j < q_tiles, scale, 1.0).astype(F32)
    o_ref[...] = (_dot(h_ref[...], wb_ref[...]) * mult).astype(BF16)


def _fox_in(h, w_stack, layer, n_out, scale, bm=1024, bn=512):
    M, D = h.shape
    return pl.pallas_call(
        functools.partial(_fox_in_kernel, scale=scale, q_tiles=D // bn),
        grid=(n_out // bn, M // bm),
        in_specs=[pl.BlockSpec((bm, D), lambda j, i: (i, 0)),
                  pl.BlockSpec((None, D, bn), lambda j, i: (layer, 0, j))],
        out_specs=pl.BlockSpec((bm, bn), lambda j, i: (i, j)),
        out_shape=jax.ShapeDtypeStruct((M, n_out), BF16),
        scratch_shapes=[pltpu.VMEM((D, bn), BF16)],
        compiler_params=_params("parallel", "arbitrary"),
        name="fox_in",
    )(h, w_stack)


FOX_EXT = 6
CUMSUM_BLOCK = 256


def _split3(x):
    hi = x.astype(BF16)
    r1 = x - hi.astype(F32)
    mid = r1.astype(BF16)
    return hi, mid, (r1 - mid.astype(F32)).astype(BF16)


def _fox_gate_kernel(h_ref, wf_ref, bf_ref, aq_ref, ak_ref):
    seq = h_ref.shape[0]
    col = lax.broadcasted_iota(jnp.int32, wf_ref.shape, 1)
    wf = jnp.where(col < FOX_HEADS, wf_ref[...], 0.0).astype(BF16)
    logit = _dot(h_ref[...], wf) + bf_ref[...]
    lf = jnp.minimum(logit, 0.0) - jnp.log1p(jnp.exp(-jnp.abs(logit)))
    blk = CUMSUM_BLOCK
    r = lax.broadcasted_iota(jnp.int32, (blk, blk), 0)
    c = lax.broadcasted_iota(jnp.int32, (blk, blk), 1)
    tri = (c <= r).astype(BF16)
    carry = jnp.zeros((1, LANES), F32)
    parts = []
    for s0 in range(0, seq, blk):
        part = sum(_dot(tri, t) for t in _split3(lf[s0:s0 + blk])) + carry
        carry = part[blk - 1:blk, :]
        parts.append(part)
    cum = jnp.concatenate(parts, axis=0)
    terms = _split3(cum * LOG2E)
    src = lax.broadcasted_iota(jnp.int32, (LANES, LANES), 0)
    dst = lax.broadcasted_iota(jnp.int32, (LANES, LANES), 1)
    lane = lax.broadcasted_iota(jnp.int32, (1, LANES), 1)
    used = lane < FOX_HEADS * FOX_EXT
    aq = jnp.where(used & (lane % FOX_EXT >= 3), 1.0, 0.0)
    ak = jnp.where(used & (lane % FOX_EXT < 3), 1.0, 0.0)
    head = src < FOX_HEADS
    for t, term in enumerate(terms):
        aq = aq + _dot(term, (head & (dst == src * FOX_EXT + t)).astype(BF16))
        ak = ak - _dot(term, (head & (dst == src * FOX_EXT + 3 + t)).astype(BF16))
    aq_ref[...] = aq.astype(BF16)
    ak_ref[...] = ak.astype(BF16)


def _fox_gate(h, w_stack, layer, gate_col, bf_row, batch, seq):
    D = h.shape[1]
    out = jax.ShapeDtypeStruct((batch * seq, LANES), BF16)
    return pl.pallas_call(
        _fox_gate_kernel,
        grid=(batch,),
        in_specs=[pl.BlockSpec((seq, D), lambda b: (b, 0)),
                  pl.BlockSpec((None, D, LANES), lambda b: (layer, 0, gate_col // LANES)),
                  pl.BlockSpec((1, LANES), lambda b: (0, 0))],
        out_specs=[pl.BlockSpec((seq, LANES), lambda b: (b, 0))] * 2,
        out_shape=[out, out],
        compiler_params=_params("parallel"),
        name="fox_gate",
    )(h, w_stack, bf_row)


def _ffn_up_kernel(h_ref, w1_ref, w3_ref, w2_ref, g_ref, w2b_ref, w1b_ref, w3b_ref):
    @pl.when(pl.program_id(1) == 0)
    def _():
        w1b_ref[...] = w1_ref[...].astype(BF16)
        w3b_ref[...] = w3_ref[...].astype(BF16)
        w2b_ref[...] = w2_ref[...].astype(BF16)

    h = h_ref[...]
    a = _dot(h, w1b_ref[...])
    b = _dot(h, w3b_ref[...])
    g_ref[...] = (a * jax.nn.sigmoid(a) * b).astype(BF16)


def _ffn_up(h, w1_stack, w3_stack, w2_stack, layer, bm=1024, bn=512):
    M, D = h.shape
    N = w1_stack.shape[2]
    wspec = pl.BlockSpec((None, D, bn), lambda j, i: (layer, 0, j))
    return pl.pallas_call(
        _ffn_up_kernel,
        grid=(N // bn, M // bm),
        in_specs=[pl.BlockSpec((bm, D), lambda j, i: (i, 0)), wspec, wspec,
                  pl.BlockSpec((None, bn, D), lambda j, i: (layer, j, 0))],
        out_specs=[pl.BlockSpec((bm, bn), lambda j, i: (i, j)),
                   pl.BlockSpec((bn, D), lambda j, i: (j, 0))],
        out_shape=[jax.ShapeDtypeStruct((M, N), BF16), jax.ShapeDtypeStruct((N, D), BF16)],
        scratch_shapes=[pltpu.VMEM((D, bn), BF16), pltpu.VMEM((D, bn), BF16)],
        compiler_params=_params("parallel", "arbitrary"),
        name="ffn_up",
    )(h, w1_stack, w3_stack, w2_stack)


def _cast_kernel(w_ref, o_ref):
    o_ref[...] = w_ref[...].astype(BF16)


def _cast_bf16(w_stack, layer, rows=512):
    _, K, N = w_stack.shape
    return pl.pallas_call(
        _cast_kernel,
        grid=(K // rows,),
        in_specs=[pl.BlockSpec((None, rows, N), lambda r: (layer, r, 0))],
        out_specs=pl.BlockSpec((rows, N), lambda r: (r, 0)),
        out_shape=jax.ShapeDtypeStruct((K, N), BF16),
        compiler_params=_params("parallel"),
        name="cast_bf16",
    )(w_stack)


def _mm_ln_kernel(a_ref, w_ref, x_ref, gate_ref, lg_ref, lb_ref, sc_ref, sh_ref,
                  xo_ref, ho_ref, y_ref):
    i = pl.program_id(0)
    slot = i % 2

    @pl.when(i == 0)
    def _():
        y_ref[1] = jnp.zeros(y_ref.shape[1:], F32)

    z = DEEPNORM_ALPHA * x_ref[...] + (1.0 + gate_ref[...]) * y_ref[1 - slot]
    mu = jnp.mean(z, axis=-1, keepdims=True)
    zc = z - mu
    var = jnp.mean(zc * zc, axis=-1, keepdims=True)
    xn = zc * lax.rsqrt(var + LN_EPS) * lg_ref[...] + lb_ref[...]
    xo_ref[...] = xn
    ho_ref[...] = (xn * (1.0 + sc_ref[...]) + sh_ref[...]).astype(BF16)

    y_ref[slot] = _dot(a_ref[...], w_ref[...])


def _mm_ln(a, w, x2, gate, ln_g, ln_b, sc, sh, seq, bm=256):
    M, K = a.shape
    D = w.shape[1]
    n = M // bm
    per = seq // bm
    prev = lambda i: jnp.maximum(i - 1, 0)
    vec = pl.BlockSpec((None, 1, D), lambda i: (prev(i) // per, 0, 0))
    par = pl.BlockSpec((1, D), lambda i: (0, 0))
    row = pl.BlockSpec((bm, D), lambda i: (prev(i), 0))
    return pl.pallas_call(
        _mm_ln_kernel,
        grid=(n + 1,),
        in_specs=[pl.BlockSpec((bm, K), lambda i: (jnp.minimum(i, n - 1), 0)),
                  pl.BlockSpec((K, D), lambda i: (0, 0), pipeline_mode=pl.Buffered(1)),
                  row, vec, par, par, vec, vec],
        out_specs=[row, row],
        out_shape=[jax.ShapeDtypeStruct((M, D), F32), jax.ShapeDtypeStruct((M, D), BF16)],
        scratch_shapes=[pltpu.VMEM((2, bm, D), F32)],
        compiler_params=_params("arbitrary"),
        name="matmul_ln",
    )(a, w, x2, gate, ln_g, ln_b, sc, sh)


def _deinterleave_pad(w):
    ev, od = w[..., 0::2], w[..., 1::2]
    pad = jnp.zeros(w.shape[:-1] + (LANES - MLA_ROPE_DIM,), w.dtype)
    return jnp.concatenate([ev, od, pad], axis=-1)


def kernel(x, c, positions, ada_w, ada_b, ln1_g, ln1_b, ln2_g, ln2_b, ffn_w1, ffn_w3, ffn_w2,
           mla_w_down, mla_q_norm, mla_w_uq, mla_kv_norm, mla_w_uk, mla_w_uv, mla_w_o,
           fox_w_in, fox_b_f, fox_w_o):
    B, S, D = x.shape
    M = B * S
    L = ada_w.shape[0]
    H = MLA_HEADS

    c_pad = jnp.concatenate([c, jnp.zeros((8 - B, D), c.dtype)], axis=0)
    mod = _ada_mod(c_pad, ada_w, ada_b)[:, :B].reshape(L, B, 6, 1, D)
    mod = jnp.transpose(mod, (0, 2, 1, 3, 4))

    inv_freq = ROPE_THETA ** (-jnp.arange(0, MLA_ROPE_DIM, 2, dtype=F32) / MLA_ROPE_DIM)
    freq_row = jnp.concatenate(
        [inv_freq, inv_freq, jnp.zeros((LANES - MLA_ROPE_DIM,), F32)]).reshape(1, LANES)
    cos, sina, sinb = _rope_tables(positions.reshape(M, 1), freq_row)

    x2 = x.reshape(M, D)
    h = _modulate(x2, mod[0, 1], mod[0, 0], S)
    mla_scale = (MLA_NOPE_DIM + MLA_ROPE_DIM) ** -0.5 * LOG2E
    fox_scale = FOX_HEAD_DIM ** -0.5 * LOG2E
    head_col = lambda hd: hd
    shared_col = lambda hd: 0

    for i in range(L):
        j = i // 2
        if i % 2 == 0:
            QL, KL = MLA_Q_LORA, MLA_KV_LORA
            wd = mla_w_down[j]
            wq = wd[:, :QL].astype(BF16)
            wkv = wd[:, QL:QL + KL].astype(BF16)
            wpe = _deinterleave_pad(wd[:, QL + KL:]).astype(BF16)
            wuq = mla_w_uq[j].reshape(QL, H, MLA_NOPE_DIM + MLA_ROPE_DIM)
            wqn = wuq[..., :MLA_NOPE_DIM].reshape(QL, H * MLA_NOPE_DIM).astype(BF16)
            wqp = _deinterleave_pad(wuq[..., MLA_NOPE_DIM:]).reshape(QL, H * LANES).astype(BF16)
            qn, ckv, kpe = _mla_down(h, wq, wkv, wpe, mla_q_norm[j].reshape(1, QL),
                                     mla_kv_norm[j].reshape(1, KL), cos, sina, sinb)
            qnope, qpe, knope, v = _mla_up(qn, ckv, wqn, wqp, mla_w_uk[j].astype(BF16),
                                           mla_w_uv[j].astype(BF16), cos, sina, sinb, mla_scale)
            o = _attention((qnope, head_col), (qpe, head_col), (knope, head_col),
                           (kpe, shared_col), (v, head_col), batch=B, seq=S, heads=H,
                           mask_block=CHUNK, ext_group=0)
            w_o = _cast_bf16(mla_w_o, j)
        else:
            qkv = _fox_in(h, fox_w_in, j, 3 * D, fox_scale)
            bf_row = jnp.zeros((1, LANES), F32).at[0, :FOX_HEADS].set(fox_b_f[j])
            aq, ak = _fox_gate(h, fox_w_in, j, 3 * D, bf_row, B, S)
            o = _attention((qkv, head_col), (aq, shared_col),
                           (qkv, lambda hd: FOX_HEADS + hd), (ak, shared_col),
                           (qkv, lambda hd: 2 * FOX_HEADS + hd), batch=B, seq=S,
                           heads=FOX_HEADS, mask_block=1, ext_group=FOX_EXT)
            w_o = _cast_bf16(fox_w_o, j)

        x2, h = _mm_ln(o, w_o, x2, mod[i, 2], ln1_g[i].reshape(1, D), ln1_b[i].reshape(1, D),
                       mod[i, 4], mod[i, 3], S)
        g, w2b = _ffn_up(h, ffn_w1, ffn_w3, ffn_w2, i)
        nxt = min(i + 1, L - 1)
        x2, h = _mm_ln(g, w2b, x2, mod[i, 5], ln2_g[i].reshape(1, D),
                       ln2_b[i].reshape(1, D), mod[nxt, 1], mod[nxt, 0], S)
    return x2.reshape(B, S, D)
```
